```python
import math
import jax, jax.numpy as jnp
from jax import lax
import numpy as np

D_MODEL = 1024
BATCH = 8
SEQ = 2048
DEPTH = 2
DEC_BATCH = 128
DEC_SEQ = 8
PAST_LEN = 16384
PAGE_SIZE = 128

N_RET_HEADS = 4
RET_DK = D_MODEL // N_RET_HEADS
RET_DV = 2 * RET_DK
RET_QK = N_RET_HEADS * RET_DK
RET_V = N_RET_HEADS * RET_DV
RET_CHUNK = 128
ROPE_BASE = 10000.0
D_CONV = D_MODEL
CONV_WIDTH = 31
D_FF = ((8 * D_MODEL // 3 + 255) // 256) * 256
N_IN = 2 * RET_QK + 2 * RET_V + 2 * D_CONV + 2 * D_MODEL
EPS = 1e-6

kernel_name = "retnet_conformer_gated_hybrid_step"


def rms_norm(x, g):
    xf = x.astype(jnp.float32)
    y = xf * lax.rsqrt(jnp.mean(xf * xf, axis=-1, keepdims=True) + EPS)
    return (y * g.astype(jnp.float32)).astype(x.dtype)


def layer_norm(x, g, b=None):
    xf = x.astype(jnp.float32)
    mu = jnp.mean(xf, axis=-1, keepdims=True)
    xc = xf - mu
    y = xc * lax.rsqrt(jnp.mean(xc * xc, axis=-1, keepdims=True) + EPS) * g.astype(jnp.float32)
    if b is not None:
        y = y + b.astype(jnp.float32)
    return y.astype(x.dtype)


def rope(x, pos):
    half = x.shape[-1] // 2
    freqs = ROPE_BASE ** (-jnp.arange(half, dtype=jnp.float32) / half)
    ang = pos.astype(jnp.float32)[:, None] * freqs[None, :]
    cos = jnp.cos(ang)[None, :, None, :]
    sin = jnp.sin(ang)[None, :, None, :]
    xf = x.astype(jnp.float32)
    x1, x2 = xf[..., :half], xf[..., half:]
    return jnp.concatenate([x1 * cos - x2 * sin, x1 * sin + x2 * cos], axis=-1)


def ret_log_decay():
    return jnp.log1p(-jnp.exp2(-5.0 - jnp.arange(N_RET_HEADS, dtype=jnp.float32)))


def retention_chunkwise(q, k, v, r0):
    B, T, H, DK = q.shape
    DV = v.shape[-1]
    C = math.gcd(T, RET_CHUNK)
    N = T // C
    log_g = ret_log_decay()
    idx = jnp.arange(C, dtype=jnp.float32)
    diff = idx[:, None] - idx[None, :]
    dmat = jnp.where(diff >= 0, jnp.exp(log_g[:, None, None] * jnp.maximum(diff, 0.0)), 0.0)
    q_decay = jnp.exp(log_g[None, :] * (idx[:, None] + 1.0))
    k_decay = jnp.exp(log_g[None, :] * (C - 1.0 - idx[:, None]))
    chunk_decay = jnp.exp(log_g * C)

    def to_chunks(t):
        return t.reshape(B, N, C, H, t.shape[-1]).transpose(1, 0, 2, 3, 4)

    def step(r, inp):
        qc, kc, vc = inp
        s = jnp.einsum('bihd,bjhd->bhij', qc, kc) * dmat[None]
        inner = jnp.einsum('bhij,bjhe->bihe', s, vc)
        cross = jnp.einsum('bihd,bhde->bihe', qc, r) * q_decay[None, :, :, None]
        r_new = r * chunk_decay[None, :, None, None] + jnp.einsum(
            'bjhd,bjhe->bhde', kc * k_decay[None, :, :, None], vc)
        return r_new, inner + cross

    r_fin, o = lax.scan(step, r0.astype(jnp.float32), (to_chunks(q), to_chunks(k), to_chunks(v)))
    o = o.transpose(1, 0, 2, 3, 4).reshape(B, T, H, DV)
    return o, r_fin


def trunk_layer(x, r0, c0, pos, g_mix_pre, g_mix_post, g_ffn_pre, g_ffn_post, w_in, b_gate,
                ret_gn_gain, w_ret_out, conv_w, conv_b, conv_ln_g, conv_ln_b, w_conv_out, w_o,
                w_ffn_gate, w_ffn_up, w_ffn_down):
    B, T, _ = x.shape
    h = rms_norm(x, g_mix_pre)
    proj = h @ w_in
    offs = np.cumsum([RET_QK, RET_QK, RET_V, RET_V, D_CONV, D_CONV, D_MODEL]).tolist()
    q, k, v, g_ret, ca, cb, ga, gb = jnp.split(proj, offs, axis=-1)

    q = rope(q.reshape(B, T, N_RET_HEADS, RET_DK), pos)
    k = rope(k.reshape(B, T, N_RET_HEADS, RET_DK), pos) * (RET_DK ** -0.5)
    v = v.reshape(B, T, N_RET_HEADS, RET_DV).astype(jnp.float32)
    o, r_new = retention_chunkwise(q, k, v, r0)
    o = layer_norm(o, ret_gn_gain.reshape(N_RET_HEADS, RET_DV))
    o = o.reshape(B, T, RET_V).astype(x.dtype)
    y_ret = (jax.nn.silu(g_ret) * o) @ w_ret_out

    u = ca * jax.nn.sigmoid(cb)
    padded = jnp.concatenate([c0.astype(u.dtype), u], axis=1)
    c_new = padded[:, -(CONV_WIDTH - 1):, :]
    z = lax.conv_general_dilated(
        padded, conv_w[:, None, :].astype(u.dtype), window_strides=(1,), padding='VALID',
        dimension_numbers=('NWC', 'WIO', 'NWC'), feature_group_count=D_CONV) + conv_b
    z = jax.nn.silu(layer_norm(z, conv_ln_g, conv_ln_b))
    y_conv = z @ w_conv_out

    b_ga, b_gb = b_gate[:D_MODEL], b_gate[D_MODEL:]
    merged = jax.nn.sigmoid(ga + b_ga) * y_ret + jax.nn.sigmoid(gb + b_gb) * y_conv
    x = x + rms_norm(merged @ w_o, g_mix_post)

    h2 = rms_norm(x, g_ffn_pre)
    f = (jax.nn.silu(h2 @ w_ffn_gate) * (h2 @ w_ffn_up)) @ w_ffn_down
    x = x + rms_norm(f, g_ffn_post)
    return x, r_new, c_new


def setup_inputs(seed: int = 0) -> dict:
    key = jax.random.key(seed)
    ks = jax.random.split(key, 24)
    f32 = jnp.float32

    def nrm(k, shape, scale):
        return jax.random.normal(k, shape, f32) * scale

    def gain(k, shape):
        return 1.0 + 0.02 * jax.random.normal(k, shape, f32)

    return {
        "x_prompt": nrm(ks[0], (BATCH, SEQ, D_MODEL), 1.0),
        "x_sample": nrm(ks[1], (DEC_BATCH, DEC_SEQ, D_MODEL), 1.0),
        "state_ret": nrm(ks[2], (DEPTH, DEC_BATCH, N_RET_HEADS, RET_DK, RET_DV), 0.5),
        "state_conv": nrm(ks[3], (DEPTH, DEC_BATCH, CONV_WIDTH - 1, D_CONV), 0.5),
        "g_mix_pre": gain(ks[4], (DEPTH, D_MODEL)),
        "g_mix_post": gain(ks[5], (DEPTH, D_MODEL)),
        "g_ffn_pre": gain(ks[6], (DEPTH, D_MODEL)),
        "g_ffn_post": gain(ks[7], (DEPTH, D_MODEL)),
        "w_in": nrm(ks[8], (DEPTH, D_MODEL, N_IN), D_MODEL ** -0.5),
        "b_gate": nrm(ks[9], (DEPTH, 2 * D_MODEL), 0.02),
        "ret_gn_gain": gain(ks[10], (DEPTH, RET_V)),
        "w_ret_out": nrm(ks[11], (DEPTH, RET_V, D_MODEL), RET_V ** -0.5),
        "conv_w": nrm(ks[12], (DEPTH, CONV_WIDTH, D_CONV), CONV_WIDTH ** -0.5),
        "conv_b": nrm(ks[13], (DEPTH, D_CONV), 0.02),
        "conv_ln_g": gain(ks[14], (DEPTH, D_CONV)),
        "conv_ln_b": nrm(ks[15], (DEPTH, D_CONV), 0.02),
        "w_conv_out": nrm(ks[16], (DEPTH, D_CONV, D_MODEL), D_CONV ** -0.5),
        "w_o": nrm(ks[17], (DEPTH, D_MODEL, D_MODEL), D_MODEL ** -0.5),
        "w_ffn_gate": nrm(ks[18], (DEPTH, D_MODEL, D_FF), D_MODEL ** -0.5),
        "w_ffn_up": nrm(ks[19], (DEPTH, D_MODEL, D_FF), D_MODEL ** -0.5),
        "w_ffn_down": nrm(ks[20], (DEPTH, D_FF, D_MODEL), D_FF ** -0.5),
    }


def reference(x_prompt, x_sample, state_ret, state_conv, g_mix_pre, g_mix_post, g_ffn_pre,
              g_ffn_post, w_in, b_gate, ret_gn_gain, w_ret_out, conv_w, conv_b, conv_ln_g,
              conv_ln_b, w_conv_out, w_o, w_ffn_gate, w_ffn_up, w_ffn_down):
    pos_prompt = jnp.arange(SEQ, dtype=jnp.int32)
    pos_sample = PAST_LEN + jnp.arange(DEC_SEQ, dtype=jnp.int32)
    xp, xs = x_prompt, x_sample
    rp_list, cp_list, rs_list, cs_list = [], [], [], []
    for l in range(DEPTH):
        params = (g_mix_pre[l], g_mix_post[l], g_ffn_pre[l], g_ffn_post[l], w_in[l], b_gate[l],
                  ret_gn_gain[l], w_ret_out[l], conv_w[l], conv_b[l], conv_ln_g[l], conv_ln_b[l],
                  w_conv_out[l], w_o[l], w_ffn_gate[l], w_ffn_up[l], w_ffn_down[l])
        r0_p = jnp.zeros((BATCH, N_RET_HEADS, RET_DK, RET_DV), jnp.float32)
        c0_p = jnp.zeros((BATCH, CONV_WIDTH - 1, D_CONV), x_prompt.dtype)
        xp, rp, cp = trunk_layer(xp, r0_p, c0_p, pos_prompt, *params)
        xs, rs, cs = trunk_layer(xs, state_ret[l], state_conv[l], pos_sample, *params)
        rp_list.append(rp.astype(x_prompt.dtype))
        cp_list.append(cp.astype(x_prompt.dtype))
        rs_list.append(rs.astype(state_ret.dtype))
        cs_list.append(cs.astype(state_conv.dtype))
    new_state_ret_prompt = jnp.stack(rp_list, axis=0)
    new_state_conv_prompt = jnp.stack(cp_list, axis=0)
    new_state_ret_sample = jnp.stack(rs_list, axis=0)
    new_state_conv_sample = jnp.stack(cs_list, axis=0)
    return (xp, xs, new_state_ret_prompt, new_state_conv_prompt, new_state_ret_sample, new_state_conv_sample)
```

```python
import functools
import math

import numpy as np
import jax
import jax.numpy as jnp
from jax import lax
from jax.experimental import pallas as pl
from jax.experimental.pallas import tpu as pltpu

F32 = jnp.float32
BF16 = jnp.bfloat16

D_MODEL = 1024
N_HEADS = 4
DK = 256
DV = 512
HALF = DK // 2
RET_QK = N_HEADS * DK
RET_V = N_HEADS * DV
D_CONV = 1024
CONV_W = 31
D_FF = 2816
ROPE_BASE = 10000.0
EPS = 1e-6
PAST_LEN = 16384

LANES = 128
N_CBLK = D_CONV // LANES
CARRY = 32
CHUNK = 256
SAMPLE_BLOCK = 128
PAIR = 16
TM_DENSE = 256
VMEM_LIMIT = 56 * 1024 * 1024

_GAMMA = [1.0 - 2.0 ** (-5.0 - h) for h in range(N_HEADS)]


def _const_spec(shape):
    nd = len(shape)
    return pl.BlockSpec(shape, lambda *_: (0,) * nd, pipeline_mode=pl.Buffered(1))


def _rms(v):
    return v * lax.rsqrt(jnp.mean(v * v, axis=-1, keepdims=True) + EPS)


def _layer_norm(v):
    mu = jnp.mean(v, axis=-1, keepdims=True)
    vc = v - mu
    return vc * lax.rsqrt(jnp.mean(vc * vc, axis=-1, keepdims=True) + EPS)


def _silu(v):
    return v * jax.nn.sigmoid(v)


_OFF_Q = 0
_OFF_V = _OFF_Q + RET_QK
_OFF_G = _OFF_V + RET_V
_OFF_CA = _OFF_G + RET_V
_OFF_CB = _OFF_CA + D_CONV
_OFF_GA = _OFF_CB + D_CONV
_OFF_GB = _OFF_GA + D_MODEL
_N_PACK = _OFF_GB + D_MODEL


def _inproj_body(x_ref, g_ref, w_ref, wkt_ref, cos_ref, sin_ref, cost_ref, sint_ref, bg_ref,
                 q_ref, kt_ref, v_ref, sg_ref, u_ref, ga_ref, gb_ref):
    x = x_ref[...]
    h = (_rms(x) * g_ref[...]).astype(BF16)

    def proj(lo, width):
        return jnp.dot(h, w_ref[:, lo:lo + width], preferred_element_type=F32)

    cos = cos_ref[...]
    sin = sin_ref[...]
    cost = cost_ref[...]
    sint = sint_ref[...]
    for hh in range(N_HEADS):
        p = proj(_OFF_Q + hh * DK, DK)
        x1 = p[:, :HALF]
        x2 = p[:, HALF:]
        q_ref[:, hh * DK:hh * DK + HALF] = (x1 * cos - x2 * sin).astype(BF16)
        q_ref[:, hh * DK + HALF:(hh + 1) * DK] = (x1 * sin + x2 * cos).astype(BF16)
        pt = lax.dot_general(wkt_ref[hh * DK:(hh + 1) * DK, :], h, (((1,), (1,)), ((), ())),
                             preferred_element_type=F32)
        y1 = pt[:HALF, :]
        y2 = pt[HALF:, :]
        scale = DK ** -0.5
        kt_ref[hh * DK:hh * DK + HALF, :] = ((y1 * cost - y2 * sint) * scale).astype(BF16)
        kt_ref[hh * DK + HALF:(hh + 1) * DK, :] = ((y1 * sint + y2 * cost) * scale).astype(BF16)
    for c in range(RET_V // DV):
        v_ref[:, c * DV:(c + 1) * DV] = proj(_OFF_V + c * DV, DV).astype(BF16)
        sg_ref[:, c * DV:(c + 1) * DV] = _silu(proj(_OFF_G + c * DV, DV)).astype(BF16)
    for c in range(D_CONV // DV):
        ca = proj(_OFF_CA + c * DV, DV)
        cb = proj(_OFF_CB + c * DV, DV)
        u_ref[:, c * DV:(c + 1) * DV] = ca * jax.nn.sigmoid(cb)
        ga = proj(_OFF_GA + c * DV, DV) + bg_ref[:, c * DV:(c + 1) * DV]
        ga_ref[:, c * DV:(c + 1) * DV] = jax.nn.sigmoid(ga).astype(BF16)
        gb = proj(_OFF_GB + c * DV, DV) + bg_ref[:, D_MODEL + c * DV:D_MODEL + (c + 1) * DV]
        gb_ref[:, c * DV:(c + 1) * DV] = jax.nn.sigmoid(gb).astype(BF16)


def _inproj(x2d, g_pre, w_pack, wkt, cos, sin, cost, sint, b_gate, n_pos_tiles):
    ntok = x2d.shape[0]
    tm = TM_DENSE
    grid = (ntok // tm,)
    tok = lambda w: pl.BlockSpec((tm, w), lambda i: (i, 0))
    pos = pl.BlockSpec((tm, HALF), lambda i: (i % n_pos_tiles, 0))
    pos_t = pl.BlockSpec((HALF, tm), lambda i: (0, i % n_pos_tiles))
    out_shape = (
        jax.ShapeDtypeStruct((ntok, RET_QK), BF16),
        jax.ShapeDtypeStruct((RET_QK, ntok), BF16),
        jax.ShapeDtypeStruct((ntok, RET_V), BF16),
        jax.ShapeDtypeStruct((ntok, RET_V), BF16),
        jax.ShapeDtypeStruct((ntok, D_CONV), F32),
        jax.ShapeDtypeStruct((ntok, D_MODEL), BF16),
        jax.ShapeDtypeStruct((ntok, D_MODEL), BF16),
    )
    out_specs = (tok(RET_QK), pl.BlockSpec((RET_QK, tm), lambda i: (0, i)), tok(RET_V), tok(RET_V),
                 tok(D_CONV), tok(D_MODEL), tok(D_MODEL))
    return pl.pallas_call(
        _inproj_body,
        grid=grid,
        in_specs=[tok(D_MODEL), _const_spec((1, D_MODEL)), _const_spec((D_MODEL, _N_PACK)),
                  _const_spec((RET_QK, D_MODEL)), pos, pos, pos_t, pos_t,
                  _const_spec((1, 2 * D_MODEL))],
        out_specs=out_specs,
        out_shape=out_shape,
        compiler_params=pltpu.CompilerParams(dimension_semantics=("arbitrary",),
                                             vmem_limit_bytes=VMEM_LIMIT),
        name="inproj",
    )(x2d, g_pre, w_pack, wkt, cos, sin, cost, sint, b_gate)


def _group_norm_gate(o, gain, sg):
    return (_layer_norm(o) * gain * sg.astype(F32)).astype(BF16)


def _conv_finish(z, cb_ref, lng_ref, lnb_ref):
    z = _layer_norm(z + cb_ref[...]) * lng_ref[...] + lnb_ref[...]
    return _silu(z).astype(BF16)


def _mix_prompt_body(q_ref, kt_ref, v_ref, sg_ref, u_ref, dm_ref, qd_ref, kd_ref, gn_ref,
                     cw_ref, cb_ref, lng_ref, lnb_ref,
                     og_ref, zb_ref, rout_ref, cout_ref, r_scr, win_scr, z_scr):
    t = pl.program_id(1)
    n_t = pl.num_programs(1)

    @pl.when(t == 0)
    def _():
        r_scr[...] = jnp.zeros_like(r_scr)
        win_scr[:, 0:CARRY, :] = jnp.zeros((N_CBLK, CARRY, LANES), F32)

    for h in range(N_HEADS):
        qh = q_ref[:, h * DK:(h + 1) * DK]
        kth = kt_ref[h * DK:(h + 1) * DK, :]
        vh = v_ref[:, h * DV:(h + 1) * DV]
        s = jnp.dot(qh, kth, preferred_element_type=F32) * dm_ref[h]
        inner = jnp.dot(s.astype(BF16), vh, preferred_element_type=F32)
        r = r_scr[h]
        cross = jnp.dot(qh, r.astype(BF16), preferred_element_type=F32) * qd_ref[h]
        ktd = (kth.astype(F32) * kd_ref[h]).astype(BF16)
        r_scr[h] = r * (_GAMMA[h] ** CHUNK) + jnp.dot(ktd, vh, preferred_element_type=F32)
        og_ref[:, h * DV:(h + 1) * DV] = _group_norm_gate(
            inner + cross, gn_ref[:, h * DV:(h + 1) * DV], sg_ref[:, h * DV:(h + 1) * DV])

    for c in range(N_CBLK):
        win_scr[c, CARRY:CARRY + CHUNK, :] = u_ref[:, c * LANES:(c + 1) * LANES]

    def conv_block(c, carry):
        acc = jnp.zeros((CHUNK, LANES), F32)
        for j in range(CONV_W):
            lo = CARRY - (CONV_W - 1) + j
            acc = acc + cw_ref[c, j:j + 1, :] * win_scr[c, lo:lo + CHUNK, :]
        z_scr[c] = acc
        return carry

    lax.fori_loop(0, N_CBLK, conv_block, 0)
    z = jnp.concatenate([z_scr[c] for c in range(N_CBLK)], axis=-1)
    zb_ref[...] = _conv_finish(z, cb_ref, lng_ref, lnb_ref)

    @pl.when(t == n_t - 1)
    def _():
        rout_ref[0] = r_scr[...]
        for c in range(N_CBLK):
            cout_ref[0, :, c * LANES:(c + 1) * LANES] = win_scr[c, CHUNK + CARRY - (CONV_W - 1):CHUNK + CARRY, :]

    win_scr[:, 0:CARRY, :] = win_scr[:, CHUNK:CHUNK + CARRY, :]


def _mix_prompt(q, kt, v, sg, u, tabs, gn, cw3, conv_b, ln_g, ln_b, batch, seq):
    n_t = seq // CHUNK
    tok = lambda w: pl.BlockSpec((CHUNK, w), lambda b, t: (b * n_t + t, 0))
    out_shape = (
        jax.ShapeDtypeStruct((batch * seq, RET_V), BF16),
        jax.ShapeDtypeStruct((batch * seq, D_CONV), BF16),
        jax.ShapeDtypeStruct((batch, N_HEADS, DK, DV), F32),
        jax.ShapeDtypeStruct((batch, CONV_W - 1, D_CONV), F32),
    )
    out_specs = (tok(RET_V), tok(D_CONV),
                 pl.BlockSpec((1, N_HEADS, DK, DV), lambda b, t: (b, 0, 0, 0)),
                 pl.BlockSpec((1, CONV_W - 1, D_CONV), lambda b, t: (b, 0, 0)))
    dm, qd, kd = tabs
    return pl.pallas_call(
        _mix_prompt_body,
        grid=(batch, n_t),
        in_specs=[tok(RET_QK), pl.BlockSpec((RET_QK, CHUNK), lambda b, t: (0, b * n_t + t)),
                  tok(RET_V), tok(RET_V), tok(D_CONV),
                  _const_spec(dm.shape), _const_spec(qd.shape), _const_spec(kd.shape),
                  _const_spec((1, RET_V)), _const_spec(cw3.shape), _const_spec((1, D_CONV)),
                  _const_spec((1, D_CONV)), _const_spec((1, D_CONV))],
        out_specs=out_specs,
        out_shape=out_shape,
        scratch_shapes=[pltpu.VMEM((N_HEADS, DK, DV), F32),
                        pltpu.VMEM((N_CBLK, CHUNK + CARRY, LANES), F32),
                        pltpu.VMEM((N_CBLK, CHUNK, LANES), F32)],
        compiler_params=pltpu.CompilerParams(dimension_semantics=("arbitrary", "arbitrary"),
                                             vmem_limit_bytes=VMEM_LIMIT),
        name="mix_prompt",
    )(q, kt, v, sg, u, dm, qd, kd, gn, cw3, conv_b, ln_g, ln_b)


def _mix_sample_body(layer, q_ref, kt_ref, v_ref, sg_ref, u_ref, r0_ref, c0_ref, dm_ref, km_ref, qd_ref,
                     gn_ref, cw_ref, cb_ref, lng_ref, lnb_ref,
                     og_ref, zb_ref, rout_ref, cout_ref, pad_scr):
    del layer
    p = pl.program_id(1)
    seq_len = PAIR // 2
    dm = dm_ref[p]
    km = km_ref[p]
    for h in range(N_HEADS):
        q16 = q_ref[:, h * DK:(h + 1) * DK]
        kth = kt_ref[h * DK:(h + 1) * DK, :]
        vh = v_ref[:, h * DV:(h + 1) * DV]
        s = jnp.dot(q16, kth, preferred_element_type=F32) * dm[h * PAIR:(h + 1) * PAIR, :]
        inner = jnp.dot(s.astype(BF16), vh, preferred_element_type=F32)
        kth32 = kth.astype(F32)
        cross = []
        for e in range(2):
            r = r0_ref[0, e, h]
            cr = jnp.dot(q16, r.astype(BF16), preferred_element_type=F32)
            cross.append(cr[e * seq_len:(e + 1) * seq_len, :])
            ktd = (kth32 * km[2 * h + e:2 * h + e + 1, :]).astype(BF16)
            rout_ref[0, e, h] = r * (_GAMMA[h] ** seq_len) + jnp.dot(ktd, vh, preferred_element_type=F32)
        o = inner + jnp.concatenate(cross, axis=0) * qd_ref[h]
        og_ref[:, h * DV:(h + 1) * DV] = _group_norm_gate(
            o, gn_ref[:, h * DV:(h + 1) * DV], sg_ref[:, h * DV:(h + 1) * DV])

    zs = []
    for e in range(2):
        pad_scr[e, 0:CONV_W - 1, :] = c0_ref[0, e]
        pad_scr[e, CONV_W - 1:CONV_W - 1 + seq_len, :] = u_ref[e * seq_len:(e + 1) * seq_len, :]
        acc = jnp.zeros((seq_len, D_CONV), F32)
        for j in range(CONV_W):
            acc = acc + cw_ref[j:j + 1, :] * pad_scr[e, j:j + seq_len, :]
        zs.append(acc)
        cout_ref[0, e] = pad_scr[e, seq_len:seq_len + CONV_W - 1, :]
    zb_ref[...] = _conv_finish(jnp.concatenate(zs, axis=0), cb_ref, lng_ref, lnb_ref)


def _mix_sample(layer, q, kt, v, sg, u, state_ret, state_conv, tabs, gn, cw, conv_b, ln_g, ln_b):
    n_layers, n_seq = state_ret.shape[0], state_ret.shape[1]
    seq_len = PAIR // 2
    ntok = n_seq * seq_len
    n_blk = ntok // SAMPLE_BLOCK
    n_pair = SAMPLE_BLOCK // PAIR
    tok = lambda w: pl.BlockSpec((PAIR, w), lambda i, p: (i * n_pair + p, 0))
    dm, km, qd = tabs
    out_shape = (
        jax.ShapeDtypeStruct((ntok, RET_V), BF16),
        jax.ShapeDtypeStruct((ntok, D_CONV), BF16),
        jax.ShapeDtypeStruct((1, n_seq, N_HEADS, DK, DV), F32),
        jax.ShapeDtypeStruct((1, n_seq, CONV_W - 1, D_CONV), F32),
    )
    out_specs = (tok(RET_V), tok(D_CONV),
                 pl.BlockSpec((1, 2, N_HEADS, DK, DV), lambda i, p: (0, i * n_pair + p, 0, 0, 0)),
                 pl.BlockSpec((1, 2, CONV_W - 1, D_CONV), lambda i, p: (0, i * n_pair + p, 0, 0)))
    return pl.pallas_call(
        functools.partial(_mix_sample_body, layer),
        grid=(n_blk, n_pair),
        in_specs=[tok(RET_QK), pl.BlockSpec((RET_QK, SAMPLE_BLOCK), lambda i, p: (0, i)),
                  pl.BlockSpec((SAMPLE_BLOCK, RET_V), lambda i, p: (i, 0)), tok(RET_V), tok(D_CONV),
                  pl.BlockSpec((1, 2, N_HEADS, DK, DV), lambda i, p: (layer, i * n_pair + p, 0, 0, 0)),
                  pl.BlockSpec((1, 2, CONV_W - 1, D_CONV), lambda i, p: (layer, i * n_pair + p, 0, 0)),
                  _const_spec(dm.shape), _const_spec(km.shape), _const_spec(qd.shape),
                  _const_spec((1, RET_V)), _const_spec((CONV_W, D_CONV)), _const_spec((1, D_CONV)),
                  _const_spec((1, D_CONV)), _const_spec((1, D_CONV))],
        out_specs=out_specs,
        out_shape=out_shape,
        scratch_shapes=[pltpu.VMEM((2, 40, D_CONV), F32)],
        compiler_params=pltpu.CompilerParams(dimension_semantics=("arbitrary", "arbitrary"),
                                             vmem_limit_bytes=VMEM_LIMIT),
        name="mix_sample",
    )(q, kt, v, sg, u, state_ret, state_conv, dm, km, qd, gn, cw, conv_b, ln_g, ln_b)


def _tail_body(og_ref, zb_ref, ga_ref, gb_ref, x_ref, wro_ref, wco_ref, wo_ref, gpost_ref,
               gfpre_ref, wg_ref, wu_ref, wd_ref, gfpost_ref, out_ref):
    y_ret = jnp.dot(og_ref[...], wro_ref[...], preferred_element_type=F32)
    y_conv = jnp.dot(zb_ref[...], wco_ref[...], preferred_element_type=F32)
    merged = (ga_ref[...].astype(F32) * y_ret + gb_ref[...].astype(F32) * y_conv).astype(BF16)
    m = jnp.dot(merged, wo_ref[...], preferred_element_type=F32)
    x1 = x_ref[...] + _rms(m) * gpost_ref[...]
    h2 = (_rms(x1) * gfpre_ref[...]).astype(BF16)
    gate = jnp.dot(h2, wg_ref[...], preferred_element_type=F32)
    up = jnp.dot(h2, wu_ref[...], preferred_element_type=F32)
    act = (_silu(gate) * up).astype(BF16)
    f = jnp.dot(act, wd_ref[...], preferred_element_type=F32)
    out_ref[...] = x1 + _rms(f) * gfpost_ref[...]


def _tail(og, zb, ga, gb, x2d, wro, wco, wo, g_post, g_fpre, wg, wu, wd, g_fpost):
    ntok = x2d.shape[0]
    tm = TM_DENSE
    tok = lambda w: pl.BlockSpec((tm, w), lambda i: (i, 0))
    return pl.pallas_call(
        _tail_body,
        grid=(ntok // tm,),
        in_specs=[tok(RET_V), tok(D_CONV), tok(D_MODEL), tok(D_MODEL), tok(D_MODEL),
                  _const_spec((RET_V, D_MODEL)), _const_spec((D_CONV, D_MODEL)),
                  _const_spec((D_MODEL, D_MODEL)), _const_spec((1, D_MODEL)), _const_spec((1, D_MODEL)),
                  _const_spec((D_MODEL, D_FF)), _const_spec((D_MODEL, D_FF)), _const_spec((D_FF, D_MODEL)),
                  _const_spec((1, D_MODEL))],
        out_specs=tok(D_MODEL),
        out_shape=jax.ShapeDtypeStruct((ntok, D_MODEL), F32),
        compiler_params=pltpu.CompilerParams(dimension_semantics=("arbitrary",),
                                             vmem_limit_bytes=VMEM_LIMIT),
        name="tail",
    )(og, zb, ga, gb, x2d, wro, wco, wo, g_post, g_fpre, wg, wu, wd, g_fpost)


def _rope_tables(pos):
    freqs = ROPE_BASE ** (-jnp.arange(HALF, dtype=F32) / HALF)
    ang = pos.astype(F32)[:, None] * freqs[None, :]
    return jnp.cos(ang), jnp.sin(ang)


def _prompt_decay_tables():
    idx = np.arange(CHUNK, dtype=np.float64)
    diff = idx[:, None] - idx[None, :]
    dm = np.stack([np.where(diff >= 0, g ** np.maximum(diff, 0.0), 0.0) for g in _GAMMA])
    qd = np.stack([g ** (idx + 1.0) for g in _GAMMA])[:, :, None]
    kd = np.stack([g ** (CHUNK - 1.0 - idx) for g in _GAMMA])[:, None, :]
    return jnp.asarray(dm, F32), jnp.asarray(qd, F32), jnp.asarray(kd, F32)


def _sample_decay_tables():
    seq_len = PAIR // 2
    n_pair = SAMPLE_BLOCK // PAIR
    tok = np.arange(SAMPLE_BLOCK)
    dm = np.zeros((n_pair, N_HEADS, PAIR, SAMPLE_BLOCK))
    km = np.zeros((n_pair, N_HEADS, 2, SAMPLE_BLOCK))
    qd = np.zeros((N_HEADS, PAIR, 1))
    for h, g in enumerate(_GAMMA):
        qd[h, :, 0] = g ** ((np.arange(PAIR) % seq_len) + 1.0)
        for p in range(n_pair):
            for i in range(PAIR):
                row_tok = p * PAIR + i
                same = (tok // seq_len == row_tok // seq_len) & (tok <= row_tok)
                dm[p, h, i] = np.where(same, g ** np.maximum(row_tok - tok, 0).astype(np.float64), 0.0)
            for e in range(2):
                mine = tok // seq_len == 2 * p + e
                km[p, h, e] = np.where(mine, g ** (seq_len - 1.0 - tok % seq_len), 0.0)
    return (jnp.asarray(dm.reshape(n_pair, N_HEADS * PAIR, SAMPLE_BLOCK), F32),
            jnp.asarray(km.reshape(n_pair, N_HEADS * 2, SAMPLE_BLOCK), F32),
            jnp.asarray(qd, F32))


def kernel(x_prompt, x_sample, state_ret, state_conv, g_mix_pre, g_mix_post, g_ffn_pre, g_ffn_post,
           w_in, b_gate, ret_gn_gain, w_ret_out, conv_w, conv_b, conv_ln_g, conv_ln_b, w_conv_out,
           w_o, w_ffn_gate, w_ffn_up, w_ffn_down):
    batch, seq, _ = x_prompt.shape
    n_seq, dec_seq, _ = x_sample.shape
    depth = w_in.shape[0]
    assert dec_seq * 2 == PAIR and seq % CHUNK == 0 and (n_seq * dec_seq) % SAMPLE_BLOCK == 0

    cos_p, sin_p = _rope_tables(jnp.arange(seq, dtype=jnp.int32))
    cos_s, sin_s = _rope_tables(PAST_LEN + jnp.arange(dec_seq, dtype=jnp.int32))
    cos_s = jnp.tile(cos_s, (TM_DENSE // dec_seq, 1))
    sin_s = jnp.tile(sin_s, (TM_DENSE // dec_seq, 1))
    rope_p = (cos_p, sin_p, cos_p.T, sin_p.T)
    rope_s = (cos_s, sin_s, cos_s.T, sin_s.T)
    tabs_p = _prompt_decay_tables()
    tabs_s = _sample_decay_tables()

    xp = x_prompt.reshape(batch * seq, D_MODEL)
    xs = x_sample.reshape(n_seq * dec_seq, D_MODEL)
    row = lambda a: a.reshape(1, -1)
    rp, cp, rs, cs = [], [], [], []
    for l in range(depth):
        w = w_in[l]
        w_pack = jnp.concatenate([w[:, :RET_QK], w[:, 2 * RET_QK:]], axis=1).astype(BF16)
        wkt = w[:, RET_QK:2 * RET_QK].T.astype(BF16)
        tail_w = (w_ret_out[l].astype(BF16), w_conv_out[l].astype(BF16), w_o[l].astype(BF16),
                  row(g_mix_post[l]), row(g_ffn_pre[l]), w_ffn_gate[l].astype(BF16),
                  w_ffn_up[l].astype(BF16), w_ffn_down[l].astype(BF16), row(g_ffn_post[l]))
        gn = row(ret_gn_gain[l])
        conv_args = (row(conv_b[l]), row(conv_ln_g[l]), row(conv_ln_b[l]))
        cw3 = conv_w[l].reshape(CONV_W, N_CBLK, LANES).transpose(1, 0, 2)

        q, kt, v, sg, u, ga, gb = _inproj(xp, row(g_mix_pre[l]), w_pack, wkt, *rope_p, row(b_gate[l]),
                                          seq // TM_DENSE)
        og, zb, r_new, c_new = _mix_prompt(q, kt, v, sg, u, tabs_p, gn, cw3, *conv_args, batch, seq)
        xp = _tail(og, zb, ga, gb, xp, *tail_w)
        rp.append(r_new)
        cp.append(c_new)

        q, kt, v, sg, u, ga, gb = _inproj(xs, row(g_mix_pre[l]), w_pack, wkt, *rope_s, row(b_gate[l]), 1)
        og, zb, r_new, c_new = _mix_sample(l, q, kt, v, sg, u, state_ret, state_conv, tabs_s, gn,
                                           conv_w[l], *conv_args)
        xs = _tail(og, zb, ga, gb, xs, *tail_w)
        rs.append(r_new[0])
        cs.append(c_new[0])

    return (xp.reshape(batch, seq, D_MODEL), xs.reshape(n_seq, dec_seq, D_MODEL),
            jnp.stack(rp), jnp.stack(cp), jnp.stack(rs), jnp.stack(cs))
```

```python
import functools

import numpy as np
import jax
import jax.numpy as jnp
from jax import lax
from jax.experimental import pallas as pl
from jax.experimental.pallas import tpu as pltpu

F32 = jnp.float32
BF16 = jnp.bfloat16

D_MODEL = 1024
N_HEADS = 4
DK = 256
DV = 512
HALF = DK // 2
RET_QK = N_HEADS * DK
RET_V = N_HEADS * DV
D_CONV = 1024
CONV_W = 31
D_FF = 2816
ROPE_BASE = 10000.0
EPS = 1e-6
PAST_LEN = 16384

LANES = 128
N_CBLK = D_CONV // LANES
CARRY = 32
CHUNK = 256
SAMPLE_BLOCK = 128
PAIR = 16
TM_DENSE = 512
VMEM_LIMIT = 56 * 1024 * 1024

_GAMMA = [1.0 - 2.0 ** (-5.0 - h) for h in range(N_HEADS)]

_OFF_Q = 0
_OFF_K = _OFF_Q + RET_QK
_OFF_V = _OFF_K + RET_QK
_OFF_G = _OFF_V + RET_V
_OFF_CA = _OFF_G + RET_V
_OFF_CB = _OFF_CA + D_CONV
_OFF_GA = _OFF_CB + D_CONV
_OFF_GB = _OFF_GA + D_MODEL
_N_IN = _OFF_GB + D_MODEL


def _const_spec(shape):
    nd = len(shape)
    return pl.BlockSpec(shape, lambda *_: (0,) * nd, pipeline_mode=pl.Buffered(1))


def _layer_spec(shape, layer):
    nd = len(shape)
    return pl.BlockSpec((None,) + tuple(shape), lambda *_: (layer,) + (0,) * nd,
                        pipeline_mode=pl.Buffered(1))


_ANY_SPEC = pl.BlockSpec(memory_space=pl.ANY)


def _state_out_spec(layer, depth, creates, blk_shape, idx_fn):
    if creates:
        return pl.BlockSpec((depth,) + blk_shape, lambda *g: (0,) + idx_fn(*g))
    return pl.BlockSpec((None,) + blk_shape, lambda *g: (layer,) + idx_fn(*g))


def _own_layer(ref, layer, creates):
    if not creates:
        return ref
    for other in range(ref.shape[0]):
        if other != layer:
            ref[other] = jnp.zeros(ref.shape[1:], ref.dtype)
    return ref.at[layer]


def _rms(v):
    return v * lax.rsqrt(jnp.mean(v * v, axis=-1, keepdims=True) + EPS)


def _layer_norm(v):
    mu = jnp.mean(v, axis=-1, keepdims=True)
    vc = v - mu
    return vc * lax.rsqrt(jnp.mean(vc * vc, axis=-1, keepdims=True) + EPS)


def _silu(v):
    return v * jax.nn.sigmoid(v)


def _group_norm_gate(o, gain, sg):
    return (_layer_norm(o) * gain * sg.astype(F32)).astype(BF16)


def _conv_finish(z, cb_ref, lng_ref, lnb_ref):
    z = _layer_norm(z + cb_ref[...]) * lng_ref[...] + lnb_ref[...]
    return _silu(z).astype(BF16)


def _inproj_body(fuse_conv, n_t, layer, creates, x_ref, g_ref, w_ref, wkt_ref, cos_ref, sin_ref,
                 cost_ref, sint_ref, bg_ref, *rest):
    if fuse_conv:
        (cw_ref, cb_ref, lng_ref, lnb_ref,
         q_ref, kt_ref, v_ref, sg_ref, zb_ref, ga_ref, gb_ref, cout_ref, win_scr, z_scr) = rest[-14:]
        t = pl.program_id(0) % n_t

        @pl.when(t == 0)
        def _():
            win_scr[:, 0:CARRY, :] = jnp.zeros((N_CBLK, CARRY, LANES), F32)
    else:
        q_ref, kt_ref, v_ref, sg_ref, u_ref, ga_ref, gb_ref = rest

    tm = x_ref.shape[0]
    x = x_ref[...]
    h = (_rms(x) * g_ref[...]).astype(BF16)

    def proj(lo, width):
        return jnp.dot(h, w_ref[:, lo:lo + width], preferred_element_type=F32)

    for c in range(D_CONV // DV):
        ca = proj(_OFF_CA + c * DV, DV)
        cb = proj(_OFF_CB + c * DV, DV)
        u = ca * jax.nn.sigmoid(cb)
        if fuse_conv:
            for cc in range(DV // LANES):
                win_scr[c * (DV // LANES) + cc, CARRY:CARRY + tm, :] = u[:, cc * LANES:(cc + 1) * LANES]
        else:
            u_ref[:, c * DV:(c + 1) * DV] = u

    if fuse_conv:
        def conv_block(c, carry):
            lo = CARRY - (CONV_W - 1)
            acc = cw_ref[c, 0:1, :] * win_scr[c, lo:lo + tm, :]
            for j in range(1, CONV_W):
                acc = acc + cw_ref[c, j:j + 1, :] * win_scr[c, lo + j:lo + j + tm, :]
            z_scr[c] = acc
            return carry

        lax.fori_loop(0, N_CBLK, conv_block, 0)
        z = jnp.concatenate([z_scr[c] for c in range(N_CBLK)], axis=-1)
        zb_ref[...] = _conv_finish(z, cb_ref, lng_ref, lnb_ref)

    cos = cos_ref[...]
    sin = sin_ref[...]
    cost = cost_ref[...]
    sint = sint_ref[...]
    for hh in range(N_HEADS):
        p = proj(_OFF_Q + hh * DK, DK)
        x1 = p[:, :HALF]
        x2 = p[:, HALF:]
        q_ref[:, hh * DK:hh * DK + HALF] = (x1 * cos - x2 * sin).astype(BF16)
        q_ref[:, hh * DK + HALF:(hh + 1) * DK] = (x1 * sin + x2 * cos).astype(BF16)
        pt = lax.dot_general(wkt_ref[hh * DK:(hh + 1) * DK, :], h, (((1,), (1,)), ((), ())),
                             preferred_element_type=F32)
        y1 = pt[:HALF, :]
        y2 = pt[HALF:, :]
        scale = DK ** -0.5
        kt_ref[hh * DK:hh * DK + HALF, :] = ((y1 * cost - y2 * sint) * scale).astype(BF16)
        kt_ref[hh * DK + HALF:(hh + 1) * DK, :] = ((y1 * sint + y2 * cost) * scale).astype(BF16)
    for c in range(RET_V // DV):
        v_ref[:, c * DV:(c + 1) * DV] = proj(_OFF_V + c * DV, DV).astype(BF16)
        sg_ref[:, c * DV:(c + 1) * DV] = _silu(proj(_OFF_G + c * DV, DV)).astype(BF16)
    for c in range(D_MODEL // DV):
        ga = proj(_OFF_GA + c * DV, DV) + bg_ref[:, c * DV:(c + 1) * DV]
        ga_ref[:, c * DV:(c + 1) * DV] = jax.nn.sigmoid(ga).astype(BF16)
        gb = proj(_OFF_GB + c * DV, DV) + bg_ref[:, D_MODEL + c * DV:D_MODEL + (c + 1) * DV]
        gb_ref[:, c * DV:(c + 1) * DV] = jax.nn.sigmoid(gb).astype(BF16)

    if fuse_conv:
        @pl.when(t == n_t - 1)
        def _():
            cout = _own_layer(cout_ref, layer, creates)
            for c in range(N_CBLK):
                cout[:, c * LANES:(c + 1) * LANES] = win_scr[c, tm + CARRY - (CONV_W - 1):tm + CARRY, :]

        win_scr[:, 0:CARRY, :] = win_scr[:, tm:tm + CARRY, :]


def _inproj(layer, x2d, g_pre, w_bf, wkt, rope, b_gate, n_t, conv=None):
    ntok = x2d.shape[0]
    tm = TM_DENSE
    cos, sin, cost, sint = rope
    tok = lambda w: pl.BlockSpec((tm, w), lambda i: (i, 0))
    pos = pl.BlockSpec((tm, HALF), lambda i: (i % n_t, 0))
    pos_t = pl.BlockSpec((HALF, tm), lambda i: (0, i % n_t))
    in_specs = [tok(D_MODEL), _layer_spec((1, D_MODEL), layer), _layer_spec((D_MODEL, _N_IN), layer),
                _layer_spec((RET_QK, D_MODEL), layer), pos, pos, pos_t, pos_t,
                _layer_spec((1, 2 * D_MODEL), layer)]
    args = [x2d, g_pre, w_bf, wkt, cos, sin, cost, sint, b_gate]
    out_shape = [
        jax.ShapeDtypeStruct((ntok, RET_QK), BF16),
        jax.ShapeDtypeStruct((RET_QK, ntok), BF16),
        jax.ShapeDtypeStruct((ntok, RET_V), BF16),
        jax.ShapeDtypeStruct((ntok, RET_V), BF16),
        None,
        jax.ShapeDtypeStruct((ntok, D_MODEL), BF16),
        jax.ShapeDtypeStruct((ntok, D_MODEL), BF16),
    ]
    out_specs = [tok(RET_QK), pl.BlockSpec((RET_QK, tm), lambda i: (0, i)), tok(RET_V), tok(RET_V),
                 tok(D_CONV), tok(D_MODEL), tok(D_MODEL)]
    scratch, aliases = [], {}
    if conv is None:
        out_shape[4] = jax.ShapeDtypeStruct((ntok, D_CONV), F32)
    else:
        conv_w, conv_b, ln_g, ln_b, n_batch, c_prev = conv
        depth = conv_w.shape[0]
        out_shape[4] = jax.ShapeDtypeStruct((ntok, D_CONV), BF16)
        in_specs += [_layer_spec((N_CBLK, CONV_W, LANES), layer), _layer_spec((1, D_CONV), layer),
                     _layer_spec((1, D_CONV), layer), _layer_spec((1, D_CONV), layer)]
        args += [conv_w, conv_b, ln_g, ln_b]
        out_shape.append(jax.ShapeDtypeStruct((depth, n_batch, CONV_W - 1, D_CONV), F32))
        out_specs.append(_state_out_spec(layer, depth, c_prev is None, (None, CONV_W - 1, D_CONV),
                                         lambda i: (i // n_t, 0, 0)))
        scratch = [pltpu.VMEM((N_CBLK, tm + CARRY, LANES), F32), pltpu.VMEM((N_CBLK, tm, LANES), F32)]
        if c_prev is not None:
            in_specs.insert(9, _ANY_SPEC)
            args.insert(9, c_prev)
            aliases = {9: 7}
    return pl.pallas_call(
        functools.partial(_inproj_body, conv is not None, n_t, layer, not aliases),
        grid=(ntok // tm,),
        in_specs=in_specs,
        out_specs=out_specs,
        out_shape=out_shape,
        scratch_shapes=scratch,
        input_output_aliases=aliases,
        compiler_params=pltpu.CompilerParams(dimension_semantics=("arbitrary",),
                                             vmem_limit_bytes=VMEM_LIMIT),
        name="inproj",
    )(*args)


def _mix_prompt_body(layer, creates, q_ref, kt_ref, v_ref, sg_ref, dm_ref, qd_ref, kd_ref, gn_ref, *rest):
    og_ref, rout_ref, r_scr = rest[-3:]
    t = pl.program_id(1)

    @pl.when(t == 0)
    def _():
        r_scr[...] = jnp.zeros_like(r_scr)

    for h in range(N_HEADS):
        qh = q_ref[:, h * DK:(h + 1) * DK]
        kth = kt_ref[h * DK:(h + 1) * DK, :]
        vh = v_ref[:, h * DV:(h + 1) * DV]
        s = jnp.dot(qh, kth, preferred_element_type=F32) * dm_ref[h]
        inner = jnp.dot(s.astype(BF16), vh, preferred_element_type=F32)
        r = r_scr[h]
        cross = jnp.dot(qh, r.astype(BF16), preferred_element_type=F32) * qd_ref[h]
        ktd = (kth.astype(F32) * kd_ref[h]).astype(BF16)
        r_scr[h] = r * (_GAMMA[h] ** CHUNK) + jnp.dot(ktd, vh, preferred_element_type=F32)
        og_ref[:, h * DV:(h + 1) * DV] = _group_norm_gate(
            inner + cross, gn_ref[:, h * DV:(h + 1) * DV], sg_ref[:, h * DV:(h + 1) * DV])

    @pl.when(t == pl.num_programs(1) - 1)
    def _():
        _own_layer(rout_ref, layer, creates)[...] = r_scr[...]


def _mix_prompt(layer, q, kt, v, sg, tabs, gn, batch, seq, r_prev):
    n_t = seq // CHUNK
    depth = gn.shape[0]
    tok = lambda w: pl.BlockSpec((CHUNK, w), lambda b, t: (b * n_t + t, 0))
    dm, qd, kd = tabs
    in_specs = [tok(RET_QK), pl.BlockSpec((RET_QK, CHUNK), lambda b, t: (0, b * n_t + t)),
                tok(RET_V), tok(RET_V),
                _const_spec(dm.shape), _const_spec(qd.shape), _const_spec(kd.shape),
                _layer_spec((1, RET_V), layer)]
    args = [q, kt, v, sg, dm, qd, kd, gn]
    aliases = {}
    if r_prev is not None:
        in_specs.append(_ANY_SPEC)
        args.append(r_prev)
        aliases = {len(args) - 1: 1}
    return pl.pallas_call(
        functools.partial(_mix_prompt_body, layer, r_prev is None),
        grid=(batch, n_t),
        in_specs=in_specs,
        out_specs=(tok(RET_V),
                   _state_out_spec(layer, depth, r_prev is None, (None, N_HEADS, DK, DV),
                                   lambda b, t: (b, 0, 0, 0))),
        out_shape=(jax.ShapeDtypeStruct((batch * seq, RET_V), BF16),
                   jax.ShapeDtypeStruct((depth, batch, N_HEADS, DK, DV), F32)),
        scratch_shapes=[pltpu.VMEM((N_HEADS, DK, DV), F32)],
        input_output_aliases=aliases,
        compiler_params=pltpu.CompilerParams(dimension_semantics=("arbitrary", "arbitrary"),
                                             vmem_limit_bytes=VMEM_LIMIT),
        name="mix_prompt",
    )(*args)


def _mix_sample_body(layer, creates, q_ref, kt_ref, v_ref, sg_ref, u_ref, r0_ref, c0_ref, dm_ref, km_ref,
                     qd_ref, gn_ref, cw_ref, cb_ref, lng_ref, lnb_ref, *rest):
    og_ref, zb_ref, rout_all, cout_all, pad_scr = rest[-5:]
    rout_ref = _own_layer(rout_all, layer, creates)
    cout_ref = _own_layer(cout_all, layer, creates)
    p = pl.program_id(1)
    seq_len = PAIR // 2
    dm = dm_ref[p]
    km = km_ref[p]
    for h in range(N_HEADS):
        q16 = q_ref[:, h * DK:(h + 1) * DK]
        kth = kt_ref[h * DK:(h + 1) * DK, :]
        vh = v_ref[:, h * DV:(h + 1) * DV]
        s = jnp.dot(q16, kth, preferred_element_type=F32) * dm[h * PAIR:(h + 1) * PAIR, :]
        inner = jnp.dot(s.astype(BF16), vh, preferred_element_type=F32)
        kth32 = kth.astype(F32)
        cross = []
        for e in range(2):
            r = r0_ref[e, h]
            cr = jnp.dot(q16, r.astype(BF16), preferred_element_type=F32)
            cross.append(cr[e * seq_len:(e + 1) * seq_len, :])
            ktd = (kth32 * km[2 * h + e:2 * h + e + 1, :]).astype(BF16)
            rout_ref[e, h] = r * (_GAMMA[h] ** seq_len) + jnp.dot(ktd, vh, preferred_element_type=F32)
        o = inner + jnp.concatenate(cross, axis=0) * qd_ref[h]
        og_ref[:, h * DV:(h + 1) * DV] = _group_norm_gate(
            o, gn_ref[:, h * DV:(h + 1) * DV], sg_ref[:, h * DV:(h + 1) * DV])

    zs = []
    for e in range(2):
        pad_scr[e, 0:CONV_W - 1, :] = c0_ref[e]
        pad_scr[e, CONV_W - 1:CONV_W - 1 + seq_len, :] = u_ref[e * seq_len:(e + 1) * seq_len, :]
        acc = jnp.zeros((seq_len, D_CONV), F32)
        for j in range(CONV_W):
            acc = acc + cw_ref[j:j + 1, :] * pad_scr[e, j:j + seq_len, :]
        zs.append(acc)
        cout_ref[e] = pad_scr[e, seq_len:seq_len + CONV_W - 1, :]
    zb_ref[...] = _conv_finish(jnp.concatenate(zs, axis=0), cb_ref, lng_ref, lnb_ref)


def _mix_sample(layer, q, kt, v, sg, u, state_ret, state_conv, tabs, gn, conv_w, conv_b, ln_g, ln_b,
                prev):
    depth, n_seq = state_ret.shape[0], state_ret.shape[1]
    seq_len = PAIR // 2
    ntok = n_seq * seq_len
    n_pair = SAMPLE_BLOCK // PAIR
    tok = lambda w: pl.BlockSpec((PAIR, w), lambda i, p: (i * n_pair + p, 0))
    ret_blk = pl.BlockSpec((None, 2, N_HEADS, DK, DV), lambda i, p: (layer, i * n_pair + p, 0, 0, 0))
    conv_blk = pl.BlockSpec((None, 2, CONV_W - 1, D_CONV), lambda i, p: (layer, i * n_pair + p, 0, 0))
    pair_idx = lambda i, p: (i * n_pair + p, 0, 0, 0)
    dm, km, qd = tabs
    in_specs = [tok(RET_QK), pl.BlockSpec((RET_QK, SAMPLE_BLOCK), lambda i, p: (0, i)),
                pl.BlockSpec((SAMPLE_BLOCK, RET_V), lambda i, p: (i, 0)), tok(RET_V), tok(D_CONV),
                ret_blk, conv_blk,
                _const_spec(dm.shape), _const_spec(km.shape), _const_spec(qd.shape),
                _layer_spec((1, RET_V), layer), _layer_spec((CONV_W, D_CONV), layer),
                _layer_spec((1, D_CONV), layer), _layer_spec((1, D_CONV), layer),
                _layer_spec((1, D_CONV), layer)]
    args = [q, kt, v, sg, u, state_ret, state_conv, dm, km, qd, gn, conv_w, conv_b, ln_g, ln_b]
    aliases = {}
    if prev is not None:
        in_specs += [_ANY_SPEC, _ANY_SPEC]
        args += list(prev)
        aliases = {len(args) - 2: 2, len(args) - 1: 3}
    return pl.pallas_call(
        functools.partial(_mix_sample_body, layer, prev is None),
        grid=(ntok // SAMPLE_BLOCK, n_pair),
        in_specs=in_specs,
        out_specs=(tok(RET_V), tok(D_CONV),
                   _state_out_spec(layer, depth, prev is None, (2, N_HEADS, DK, DV), pair_idx),
                   _state_out_spec(layer, depth, prev is None, (2, CONV_W - 1, D_CONV),
                                   lambda i, p: pair_idx(i, p)[:3])),
        out_shape=(jax.ShapeDtypeStruct((ntok, RET_V), BF16),
                   jax.ShapeDtypeStruct((ntok, D_CONV), BF16),
                   jax.ShapeDtypeStruct(state_ret.shape, F32),
                   jax.ShapeDtypeStruct(state_conv.shape, F32)),
        scratch_shapes=[pltpu.VMEM((2, 40, D_CONV), F32)],
        input_output_aliases=aliases,
        compiler_params=pltpu.CompilerParams(dimension_semantics=("arbitrary", "arbitrary"),
                                             vmem_limit_bytes=VMEM_LIMIT),
        name="mix_sample",
    )(*args)


def _tail_body(og_ref, zb_ref, ga_ref, gb_ref, x_ref, wro_ref, wco_ref, wo_ref, gpost_ref,
               gfpre_ref, wg_ref, wu_ref, wd_ref, gfpost_ref, out_ref):
    y_ret = jnp.dot(og_ref[...], wro_ref[...], preferred_element_type=F32)
    y_conv = jnp.dot(zb_ref[...], wco_ref[...], preferred_element_type=F32)
    merged = (ga_ref[...].astype(F32) * y_ret + gb_ref[...].astype(F32) * y_conv).astype(BF16)
    m = jnp.dot(merged, wo_ref[...], preferred_element_type=F32)
    x1 = x_ref[...] + _rms(m) * gpost_ref[...]
    h2 = (_rms(x1) * gfpre_ref[...]).astype(BF16)
    gate = jnp.dot(h2, wg_ref[...], preferred_element_type=F32)
    up = jnp.dot(h2, wu_ref[...], preferred_element_type=F32)
    act = (_silu(gate) * up).astype(BF16)
    f = jnp.dot(act, wd_ref[...], preferred_element_type=F32)
    out_ref[...] = x1 + _rms(f) * gfpost_ref[...]


def _tail(layer, og, zb, ga, gb, x2d, wro, wco, wo, g_post, g_fpre, wg, wu, wd, g_fpost):
    ntok = x2d.shape[0]
    tm = TM_DENSE
    tok = lambda w: pl.BlockSpec((tm, w), lambda i: (i, 0))
    vec = _layer_spec((1, D_MODEL), layer)
    return pl.pallas_call(
        _tail_body,
        grid=(ntok // tm,),
        in_specs=[tok(RET_V), tok(D_CONV), tok(D_MODEL), tok(D_MODEL), tok(D_MODEL),
                  _layer_spec((RET_V, D_MODEL), layer), _layer_spec((D_CONV, D_MODEL), layer),
                  _layer_spec((D_MODEL, D_MODEL), layer), vec, vec,
                  _layer_spec((D_MODEL, D_FF), layer), _layer_spec((D_MODEL, D_FF), layer),
                  _layer_spec((D_FF, D_MODEL), layer), vec],
        out_specs=tok(D_MODEL),
        out_shape=jax.ShapeDtypeStruct((ntok, D_MODEL), F32),
        compiler_params=pltpu.CompilerParams(dimension_semantics=("arbitrary",),
                                             vmem_limit_bytes=VMEM_LIMIT),
        name="tail",
    )(og, zb, ga, gb, x2d, wro, wco, wo, g_post, g_fpre, wg, wu, wd, g_fpost)


def _rope_tables(pos, reps):
    freqs = ROPE_BASE ** (-jnp.arange(HALF, dtype=F32) / HALF)
    ang = pos.astype(F32)[:, None] * freqs[None, :]
    cos = jnp.tile(jnp.cos(ang), (reps, 1))
    sin = jnp.tile(jnp.sin(ang), (reps, 1))
    return cos, sin, cos.T, sin.T


def _prompt_decay_tables():
    idx = np.arange(CHUNK, dtype=np.float64)
    diff = idx[:, None] - idx[None, :]
    dm = np.stack([np.where(diff >= 0, g ** np.maximum(diff, 0.0), 0.0) for g in _GAMMA])
    qd = np.stack([g ** (idx + 1.0) for g in _GAMMA])[:, :, None]
    kd = np.stack([g ** (CHUNK - 1.0 - idx) for g in _GAMMA])[:, None, :]
    return jnp.asarray(dm, F32), jnp.asarray(qd, F32), jnp.asarray(kd, F32)


def _sample_decay_tables():
    seq_len = PAIR // 2
    n_pair = SAMPLE_BLOCK // PAIR
    tok = np.arange(SAMPLE_BLOCK)
    dm = np.zeros((n_pair, N_HEADS, PAIR, SAMPLE_BLOCK))
    km = np.zeros((n_pair, N_HEADS, 2, SAMPLE_BLOCK))
    qd = np.zeros((N_HEADS, PAIR, 1))
    for h, g in enumerate(_GAMMA):
        qd[h, :, 0] = g ** ((np.arange(PAIR) % seq_len) + 1.0)
        for p in range(n_pair):
            for i in range(PAIR):
                row_tok = p * PAIR + i
                same = (tok // seq_len == row_tok // seq_len) & (tok <= row_tok)
                dm[p, h, i] = np.where(same, g ** np.maximum(row_tok - tok, 0).astype(np.float64), 0.0)
            for e in range(2):
                mine = tok // seq_len == 2 * p + e
                km[p, h, e] = np.where(mine, g ** (seq_len - 1.0 - tok % seq_len), 0.0)
    return (jnp.asarray(dm.reshape(n_pair, N_HEADS * PAIR, SAMPLE_BLOCK), F32),
            jnp.asarray(km.reshape(n_pair, N_HEADS * 2, SAMPLE_BLOCK), F32),
            jnp.asarray(qd, F32))


def kernel(x_prompt, x_sample, state_ret, state_conv, g_mix_pre, g_mix_post, g_ffn_pre, g_ffn_post,
           w_in, b_gate, ret_gn_gain, w_ret_out, conv_w, conv_b, conv_ln_g, conv_ln_b, w_conv_out,
           w_o, w_ffn_gate, w_ffn_up, w_ffn_down):
    batch, seq, _ = x_prompt.shape
    n_seq, dec_seq, _ = x_sample.shape
    depth = w_in.shape[0]
    assert dec_seq * 2 == PAIR and seq % CHUNK == 0 and (n_seq * dec_seq) % SAMPLE_BLOCK == 0

    rope_p = _rope_tables(jnp.arange(seq, dtype=jnp.int32), 1)
    rope_s = _rope_tables(PAST_LEN + jnp.arange(dec_seq, dtype=jnp.int32), TM_DENSE // dec_seq)
    tabs_p = _prompt_decay_tables()
    tabs_s = _sample_decay_tables()

    w_bf = w_in.astype(BF16)
    wkt = jnp.swapaxes(w_in[:, :, _OFF_K:_OFF_K + RET_QK], 1, 2).astype(BF16)
    tail_w = (w_ret_out.astype(BF16), w_conv_out.astype(BF16), w_o.astype(BF16))
    ffn_w = (w_ffn_gate.astype(BF16), w_ffn_up.astype(BF16), w_ffn_down.astype(BF16))
    vec = lambda a: a.reshape(depth, 1, -1)
    g_pre, g_post, g_fpre, g_fpost = vec(g_mix_pre), vec(g_mix_post), vec(g_ffn_pre), vec(g_ffn_post)
    bg, gn = vec(b_gate), vec(ret_gn_gain)
    cb, lng, lnb = vec(conv_b), vec(conv_ln_g), vec(conv_ln_b)
    cw_blocked = conv_w.reshape(depth, CONV_W, N_CBLK, LANES).transpose(0, 2, 1, 3)

    xp = x_prompt.reshape(batch * seq, D_MODEL)
    xs = x_sample.reshape(n_seq * dec_seq, D_MODEL)
    r_p = c_p = sample_states = None
    for l in range(depth):
        q, kt, v, sg, zb, ga, gb, c_p = _inproj(l, xp, g_pre, w_bf, wkt, rope_p, bg, seq // TM_DENSE,
                                                conv=(cw_blocked, cb, lng, lnb, batch, c_p))
        og, r_p = _mix_prompt(l, q, kt, v, sg, tabs_p, gn, batch, seq, r_p)
        xp = _tail(l, og, zb, ga, gb, xp, *tail_w, g_post, g_fpre, *ffn_w, g_fpost)

        q, kt, v, sg, u, ga, gb = _inproj(l, xs, g_pre, w_bf, wkt, rope_s, bg, 1)
        og, zb, r_s, c_s = _mix_sample(l, q, kt, v, sg, u, state_ret, state_conv, tabs_s, gn,
                                       conv_w, cb, lng, lnb, sample_states)
        sample_states = (r_s, c_s)
        xs = _tail(l, og, zb, ga, gb, xs, *tail_w, g_post, g_fpre, *ffn_w, g_fpost)

    return (xp.reshape(batch, seq, D_MODEL), xs.reshape(n_seq, dec_seq, D_MODEL),
            r_p, c_p, sample_states[0], sample_states[1])
```
